```python
import jax, jax.numpy as jnp
from jax import lax
import numpy as np

D_MODEL = 1024
BATCH = 8
SEQ = 2048
DEPTH = 4

HEAD_DIM = 64
CONV_HEADS = 4
ATTN_HEADS = 8
SGU_HEADS = 4
CONV_W = CONV_HEADS * HEAD_DIM
ATTN_W = ATTN_HEADS * HEAD_DIM
SGU_W = SGU_HEADS * HEAD_DIM
D_MIX = CONV_W + ATTN_W + SGU_W
D_IN_PROJ = 3 * CONV_W + 3 * ATTN_W + 2 * SGU_W
CONV_WIDTH = 3
Q_BLOCK = 128
CHUNK = 128
D_FF = 4 * D_MODEL
PLE_DIM = 256
EPS = 1e-6

kernel_name = "hybrid_conv_stickbreak_sgu_block"


def rms_norm(x, g):
    xf = x.astype(jnp.float32)
    y = xf * lax.rsqrt(jnp.mean(xf * xf, axis=-1, keepdims=True) + EPS)
    return (y * g.astype(jnp.float32)).astype(x.dtype)


def short_conv(x, w):
    s = x.shape[1]
    xp = jnp.pad(x, ((0, 0), (CONV_WIDTH - 1, 0), (0, 0)))
    out = w[0] * xp[:, 0:s]
    for j in range(1, CONV_WIDTH):
        out = out + w[j] * xp[:, j:j + s]
    return out


def stick_breaking_attention(q, k, v):
    s_len = q.shape[1]
    scale = HEAD_DIM ** -0.5
    outs = []
    for qb in range(s_len // Q_BLOCK):
        start = qb * Q_BLOCK
        end = start + Q_BLOCK
        qi = q[:, start:end].astype(jnp.float32)
        kj = k[:, :end].astype(jnp.float32)
        vj = v[:, :end].astype(jnp.float32)
        z = jnp.einsum('bqhd,bkhd->bhqk', qi, kj) * scale
        t_pos = start + jnp.arange(Q_BLOCK)[:, None]
        s_pos = jnp.arange(end)[None, :]
        causal = s_pos < t_pos
        log_beta = jax.nn.log_sigmoid(z)
        log_rem = jnp.where(causal, jax.nn.log_sigmoid(-z), 0.0)
        suffix = lax.cumsum(log_rem, axis=3, reverse=True) - log_rem
        weights = jnp.where(causal, jnp.exp(log_beta + suffix), 0.0)
        o = jnp.einsum('bhqk,bkhd->bqhd', weights, vj)
        outs.append(o.astype(v.dtype))
    return jnp.concatenate(outs, axis=1)


def spatial_gating(u, v, g_v, w_s, b_s):
    bsz, s_len, _ = u.shape
    v = v.reshape(bsz, s_len, SGU_HEADS, HEAD_DIM)
    v = rms_norm(v, g_v.reshape(SGU_HEADS, HEAD_DIM))
    v = v.reshape(bsz, s_len // CHUNK, CHUNK, SGU_HEADS, HEAD_DIM)
    mask = jnp.tril(jnp.ones((CHUNK, CHUNK), dtype=w_s.dtype))
    w = w_s * mask
    sv = jnp.einsum('gts,bcsge->bctge', w, v) + b_s.T[:, :, None]
    return u * sv.reshape(bsz, s_len, SGU_W)


def setup_inputs(seed: int = 0) -> dict:
    key = jax.random.key(seed)
    ks = jax.random.split(key, 17)

    def nrm(k, shape, scale):
        return jax.random.normal(k, shape, jnp.float32) * scale

    def gain(k, shape):
        return 1.0 + 0.05 * jax.random.normal(k, shape, jnp.float32)

    return {
        "x": nrm(ks[0], (BATCH, SEQ, D_MODEL), 1.0),
        "p": nrm(ks[1], (DEPTH, BATCH, SEQ, PLE_DIM), 1.0),
        "norm1_g": gain(ks[2], (DEPTH, D_MODEL)),
        "w_in": nrm(ks[3], (DEPTH, D_MODEL, D_IN_PROJ), D_MODEL ** -0.5),
        "conv_w": nrm(ks[4], (DEPTH, CONV_WIDTH, CONV_W), CONV_WIDTH ** -0.5),
        "q_norm_g": gain(ks[5], (DEPTH, HEAD_DIM)),
        "k_norm_g": gain(ks[6], (DEPTH, HEAD_DIM)),
        "sgu_norm_g": gain(ks[7], (DEPTH, SGU_W)),
        "sgu_w": nrm(ks[8], (DEPTH, SGU_HEADS, CHUNK, CHUNK), CHUNK ** -0.5),
        "sgu_b": gain(ks[9], (DEPTH, SGU_HEADS, CHUNK)),
        "w_out": nrm(ks[10], (DEPTH, D_MIX, D_MODEL), D_MIX ** -0.5),
        "norm2_g": gain(ks[11], (DEPTH, D_MODEL)),
        "w_ff1": nrm(ks[12], (DEPTH, D_MODEL, D_FF), D_MODEL ** -0.5),
        "w_ff2": nrm(ks[13], (DEPTH, D_FF, D_MODEL), D_FF ** -0.5),
        "norm3_g": gain(ks[14], (DEPTH, D_MODEL)),
        "w_ple_gate": nrm(ks[15], (DEPTH, D_MODEL, D_MODEL), D_MODEL ** -0.5),
        "w_ple_proj": nrm(ks[16], (DEPTH, PLE_DIM, D_MODEL), PLE_DIM ** -0.5),
    }


def reference(x, p, norm1_g, w_in, conv_w, q_norm_g, k_norm_g, sgu_norm_g, sgu_w, sgu_b,
              w_out, norm2_g, w_ff1, w_ff2, norm3_g, w_ple_gate, w_ple_proj):
    bsz, s_len, _ = x.shape
    split_idx = list(np.cumsum([CONV_W, CONV_W, CONV_W, ATTN_W, ATTN_W, ATTN_W, SGU_W]))
    h = x
    for i in range(DEPTH):
        hn = rms_norm(h, norm1_g[i])
        proj = hn @ w_in[i]
        a_b, a_c, a_h, q, k, v, c_u, c_v = jnp.split(proj, split_idx, axis=-1)
        y_a = a_b * short_conv(a_c * a_h, conv_w[i])
        q = rms_norm(q.reshape(bsz, s_len, ATTN_HEADS, HEAD_DIM), q_norm_g[i])
        k = rms_norm(k.reshape(bsz, s_len, ATTN_HEADS, HEAD_DIM), k_norm_g[i])
        v = v.reshape(bsz, s_len, ATTN_HEADS, HEAD_DIM)
        y_b = stick_breaking_attention(q, k, v).reshape(bsz, s_len, ATTN_W)
        y_c = spatial_gating(jax.nn.gelu(c_u, approximate=False),
                             jax.nn.gelu(c_v, approximate=False),
                             sgu_norm_g[i], sgu_w[i], sgu_b[i])
        h = h + jnp.concatenate([y_a, y_b, y_c], axis=-1) @ w_out[i]
        f = jnp.square(jax.nn.relu(rms_norm(h, norm2_g[i]) @ w_ff1[i]))
        h = h + f @ w_ff2[i]
        gate = jax.nn.sigmoid(rms_norm(h, norm3_g[i]) @ w_ple_gate[i])
        h = h + gate * (p[i] @ w_ple_proj[i])
    return h
```

```python
import functools

import jax
import jax.numpy as jnp
from jax import lax
from jax.experimental import pallas as pl
from jax.experimental.pallas import tpu as pltpu

HEAD_DIM = 64
CONV_HEADS = 4
ATTN_HEADS = 8
SGU_HEADS = 4
CONV_W = CONV_HEADS * HEAD_DIM
ATTN_W = ATTN_HEADS * HEAD_DIM
SGU_W = SGU_HEADS * HEAD_DIM
CONV_WIDTH = 3
CHUNK = 128
EPS = 1e-6

_OFF_AB = 0
_OFF_AC = _OFF_AB + CONV_W
_OFF_AH = _OFF_AC + CONV_W
_OFF_Q = _OFF_AH + CONV_W
_OFF_K = _OFF_Q + ATTN_W
_OFF_V = _OFF_K + ATTN_W
_OFF_CU = _OFF_V + ATTN_W
_OFF_CV = _OFF_CU + SGU_W
_OFF_END = _OFF_CV + SGU_W

V7X_LANES = 128
V7X_SUBLANES = 8
V7X_MXU_DIM = 256
V7X_VMEM_LIMIT = 56 * 1024 * 1024

TM_MIX = 512
TM_POST = 256
T_ATTN = 256
FF_CHUNK = 1024

_F32 = jnp.float32
_BF16 = jnp.bfloat16


def _rms_norm_rows(x, g):
    ms = jnp.mean(x * x, axis=-1, keepdims=True)
    return x * lax.rsqrt(ms + EPS) * g


def _group_mean_square(x):
    r = lax.broadcasted_iota(jnp.int32, (V7X_MXU_DIM, V7X_MXU_DIM), 0) // HEAD_DIM
    c = lax.broadcasted_iota(jnp.int32, (V7X_MXU_DIM, V7X_MXU_DIM), 1) // HEAD_DIM
    ones_bd = (r == c).astype(_BF16)
    x2 = (x * x).astype(_BF16)
    parts = []
    for j in range(x.shape[1] // V7X_MXU_DIM):
        parts.append(jnp.dot(x2[:, j * V7X_MXU_DIM:(j + 1) * V7X_MXU_DIM], ones_bd,
                             preferred_element_type=_F32))
    ms = parts[0] if len(parts) == 1 else jnp.concatenate(parts, axis=1)
    return ms * (1.0 / HEAD_DIM)


def _gelu(x):
    return 0.5 * x * (1.0 + lax.erf(x * (2.0 ** -0.5)))


def _mix_in_kernel(h_ref, g1_ref, win_ref, convw_ref, gq_ref, gk_ref, gv_ref, sw_ref, sb_ref,
                   ya_ref, q_ref, k_ref, v_ref, yc_ref, xs_ref):
    tm = h_ref.shape[1]
    s = pl.program_id(1)
    hn = _rms_norm_rows(h_ref[0], g1_ref[...]).astype(_BF16)

    def proj(a, b):
        return jnp.dot(hn, win_ref[:, a:b], preferred_element_type=_F32)

    a_b = proj(_OFF_AB, _OFF_AC)
    x = proj(_OFF_AC, _OFF_AH) * proj(_OFF_AH, _OFF_Q)

    @pl.when(s == 0)
    def _():
        xs_ref[0:V7X_SUBLANES, :] = jnp.zeros((V7X_SUBLANES, CONV_W), _F32)

    @pl.when(s > 0)
    def _():
        xs_ref[0:V7X_SUBLANES, :] = xs_ref[tm:tm + V7X_SUBLANES, :]

    xs_ref[V7X_SUBLANES:tm + V7X_SUBLANES, :] = x
    cw = convw_ref[...]
    conv = (cw[0:1, :] * xs_ref[V7X_SUBLANES - 2:tm + V7X_SUBLANES - 2, :]
            + cw[1:2, :] * xs_ref[V7X_SUBLANES - 1:tm + V7X_SUBLANES - 1, :]
            + cw[2:3, :] * x)
    ya_ref[0] = (a_b * conv).astype(_BF16)

    q = proj(_OFF_Q, _OFF_K)
    q_ref[0] = (q * lax.rsqrt(_group_mean_square(q) + EPS) * (gq_ref[...] * HEAD_DIM ** -0.5)).astype(_BF16)
    k = proj(_OFF_K, _OFF_V)
    k_ref[0] = (k * lax.rsqrt(_group_mean_square(k) + EPS) * gk_ref[...]).astype(_BF16)
    v_ref[0] = proj(_OFF_V, _OFF_CU).astype(_BF16)

    u = _gelu(proj(_OFF_CU, _OFF_CV))
    cv = _gelu(proj(_OFF_CV, _OFF_END))
    vn = (cv * lax.rsqrt(_group_mean_square(cv) + EPS) * gv_ref[...]).astype(_BF16)
    tr = lax.broadcasted_iota(jnp.int32, (CHUNK, CHUNK), 0)
    tc = lax.broadcasted_iota(jnp.int32, (CHUNK, CHUNK), 1)
    w_tril = [jnp.where(tr >= tc, sw_ref[g], 0.0).astype(_BF16) for g in range(SGU_HEADS)]
    lane_group = lax.broadcasted_iota(jnp.int32, (CHUNK, SGU_W), 1) // HEAD_DIM
    bias = sb_ref[...]
    for c in range(tm // CHUNK):
        rows = slice(c * CHUNK, (c + 1) * CHUNK)
        vc = vn[rows, :]
        sv = jnp.dot(w_tril[0], vc, preferred_element_type=_F32)
        for g in range(1, SGU_HEADS):
            sv = jnp.where(lane_group == g, jnp.dot(w_tril[g], vc, preferred_element_type=_F32), sv)
        yc_ref[0, rows, :] = (u[rows, :] * (sv + bias)).astype(_BF16)


def _sb_attn_kernel(q_ref, k_ref, v_ref, o_ref, acc_ref, c_ref):
    t = q_ref.shape[1]
    i = pl.program_id(2)
    q = q_ref[0]
    lane = lax.broadcasted_iota(jnp.int32, (t, V7X_LANES), 1)
    row = lax.broadcasted_iota(jnp.int32, (t, t), 0)
    col = lax.broadcasted_iota(jnp.int32, (t, t), 1)
    causal = col < row
    suffix = (row >= col).astype(_BF16)
    suffix2 = jnp.concatenate([suffix, suffix], axis=0)
    q_heads = [jnp.where(lane < HEAD_DIM, q, jnp.zeros_like(q)),
               jnp.where(lane >= HEAD_DIM, q, jnp.zeros_like(q))]

    acc_ref[...] = jnp.zeros_like(acc_ref)
    c_ref[...] = jnp.zeros_like(c_ref)

    def block(start, masked):
        kb = k_ref[0, pl.ds(start, t), :]
        vb = v_ref[0, pl.ds(start, t), :]
        for h in range(2):
            z = lax.dot_general(q_heads[h], kb, (((1,), (1,)), ((), ())), preferred_element_type=_F32)
            log_rem = -(jnp.maximum(z, 0.0) + jnp.log1p(jnp.exp(-jnp.abs(z))))
            if masked:
                log_rem = jnp.where(causal, log_rem, 0.0)
            hi = log_rem.astype(_BF16)
            lo = (log_rem - hi.astype(_F32)).astype(_BF16)
            ssum = jnp.dot(jnp.concatenate([hi, lo], axis=1), suffix2, preferred_element_type=_F32)
            c = c_ref[h]
            w = jnp.exp(z + ssum + c)
            if masked:
                w = jnp.where(causal, w, 0.0)
            acc_ref[h] += jnp.dot(w.astype(_BF16), vb, preferred_element_type=_F32)
            c_ref[h] = c + ssum[:, 0:1]

    block(pl.multiple_of(i * t, t), True)

    def body(jj, carry):
        block(pl.multiple_of((i - 1 - jj) * t, t), False)
        return carry

    lax.fori_loop(0, i, body, 0)
    o_ref[0] = jnp.where(lane < HEAD_DIM, acc_ref[0], acc_ref[1]).astype(o_ref.dtype)


def _post_kernel(h_ref, ya_ref, yb_ref, yc_ref, p_ref, wout_ref, g2_ref, w1_ref, w2_ref, g3_ref,
                 wg_ref, wp_ref, o_ref):
    y = jnp.concatenate([ya_ref[0], yb_ref[0], yc_ref[0]], axis=1)
    h = h_ref[0] + jnp.dot(y, wout_ref[...], preferred_element_type=_F32)
    hn = _rms_norm_rows(h, g2_ref[...]).astype(_BF16)
    for c in range(w1_ref.shape[1] // FF_CHUNK):
        cols = slice(c * FF_CHUNK, (c + 1) * FF_CHUNK)
        f = jnp.maximum(jnp.dot(hn, w1_ref[:, cols], preferred_element_type=_F32), 0.0)
        h = h + jnp.dot((f * f).astype(_BF16), w2_ref[cols, :], preferred_element_type=_F32)
    hn = _rms_norm_rows(h, g3_ref[...]).astype(_BF16)
    gate = jax.nn.sigmoid(jnp.dot(hn, wg_ref[...], preferred_element_type=_F32))
    emb = jnp.dot(p_ref[0].astype(_BF16), wp_ref[...], preferred_element_type=_F32)
    o_ref[0] = h + gate * emb


def _const_spec(shape):
    return pl.BlockSpec(shape, lambda *_: (0,) * len(shape), pipeline_mode=pl.Buffered(1))


def _mix_in(h, g1, w_in, conv_w, gq, gk, gv, sgu_w, sgu_bias):
    bsz, s_len, d = h.shape
    tm = TM_MIX
    tok = lambda w: pl.BlockSpec((1, tm, w), lambda b, s: (b, s, 0))
    out = lambda w: jax.ShapeDtypeStruct((bsz, s_len, w), _BF16)
    return pl.pallas_call(
        _mix_in_kernel,
        grid=(bsz, s_len // tm),
        in_specs=[tok(d), _const_spec(g1.shape), _const_spec(w_in.shape), _const_spec(conv_w.shape),
                  _const_spec(gq.shape), _const_spec(gk.shape), _const_spec(gv.shape),
                  _const_spec(sgu_w.shape), _const_spec(sgu_bias.shape)],
        out_specs=[tok(CONV_W), tok(ATTN_W), tok(ATTN_W), tok(ATTN_W), tok(SGU_W)],
        out_shape=[out(CONV_W), out(ATTN_W), out(ATTN_W), out(ATTN_W), out(SGU_W)],
        scratch_shapes=[pltpu.VMEM((tm + V7X_SUBLANES, CONV_W), _F32)],
        compiler_params=pltpu.CompilerParams(
            dimension_semantics=("arbitrary", "arbitrary"), vmem_limit_bytes=V7X_VMEM_LIMIT),
        name="mix_in",
    )(h, g1, w_in, conv_w, gq, gk, gv, sgu_w, sgu_bias)


def _sb_attn(q, k, v):
    bsz, s_len, _ = q.shape
    t = T_ATTN
    pair = 2 * HEAD_DIM
    return pl.pallas_call(
        _sb_attn_kernel,
        grid=(bsz, ATTN_W // pair, s_len // t),
        in_specs=[pl.BlockSpec((1, t, pair), lambda b, hp, i: (b, i, hp)),
                  pl.BlockSpec((1, s_len, pair), lambda b, hp, i: (b, 0, hp)),
                  pl.BlockSpec((1, s_len, pair), lambda b, hp, i: (b, 0, hp))],
        out_specs=pl.BlockSpec((1, t, pair), lambda b, hp, i: (b, i, hp)),
        out_shape=jax.ShapeDtypeStruct(q.shape, _BF16),
        scratch_shapes=[pltpu.VMEM((2, t, pair), _F32), pltpu.VMEM((2, t, 1), _F32)],
        compiler_params=pltpu.CompilerParams(
            dimension_semantics=("arbitrary", "arbitrary", "arbitrary"), vmem_limit_bytes=V7X_VMEM_LIMIT),
        name="sb_attn",
    )(q, k, v)


def _post(h, ya, yb, yc, p, w_out, g2, w1, w2, g3, wg, wp):
    bsz, s_len, d = h.shape
    tm = TM_POST
    tok = lambda w: pl.BlockSpec((1, tm, w), lambda b, s: (b, s, 0))
    return pl.pallas_call(
        _post_kernel,
        grid=(bsz, s_len // tm),
        in_specs=[tok(d), tok(CONV_W), tok(ATTN_W), tok(SGU_W), tok(p.shape[-1]),
                  _const_spec(w_out.shape), _const_spec(g2.shape), _const_spec(w1.shape),
                  _const_spec(w2.shape), _const_spec(g3.shape), _const_spec(wg.shape), _const_spec(wp.shape)],
        out_specs=tok(d),
        out_shape=jax.ShapeDtypeStruct(h.shape, h.dtype),
        compiler_params=pltpu.CompilerParams(
            dimension_semantics=("arbitrary", "arbitrary"), vmem_limit_bytes=V7X_VMEM_LIMIT),
        name="post",
    )(h, ya, yb, yc, p, w_out, g2, w1, w2, g3, wg, wp)


def kernel(x, p, norm1_g, w_in, conv_w, q_norm_g, k_norm_g, sgu_norm_g, sgu_w, sgu_b, w_out, norm2_g,
           w_ff1, w_ff2, norm3_g, w_ple_gate, w_ple_proj):
    depth = w_in.shape[0]
    assert w_in.shape[2] == _OFF_END and x.shape[1] % TM_MIX == 0 and x.shape[1] % T_ATTN == 0
    row = lambda a: a.reshape(1, -1)
    h = x
    for i in range(depth):
        gq = row(jnp.tile(q_norm_g[i], ATTN_HEADS))
        gk = row(jnp.tile(k_norm_g[i], ATTN_HEADS))
        sgu_bias = jnp.repeat(sgu_b[i].T, HEAD_DIM, axis=1)
        ya, q, k, v, yc = _mix_in(h, row(norm1_g[i]), w_in[i].astype(_BF16), conv_w[i], gq, gk,
                                  row(sgu_norm_g[i]), sgu_w[i], sgu_bias)
        yb = _sb_attn(q, k, v)
        h = _post(h, ya, yb, yc, p[i], w_out[i].astype(_BF16), row(norm2_g[i]), w_ff1[i].astype(_BF16),
                  w_ff2[i].astype(_BF16), row(norm3_g[i]), w_ple_gate[i].astype(_BF16),
                  w_ple_proj[i].astype(_BF16))
    return h
```

```python
import functools

import jax
import jax.numpy as jnp
from jax import lax
from jax.experimental import pallas as pl
from jax.experimental.pallas import tpu as pltpu

HEAD_DIM = 64
CONV_HEADS = 4
ATTN_HEADS = 8
SGU_HEADS = 4
CONV_W = CONV_HEADS * HEAD_DIM
ATTN_W = ATTN_HEADS * HEAD_DIM
SGU_W = SGU_HEADS * HEAD_DIM
CONV_WIDTH = 3
CHUNK = 128
EPS = 1e-6

_OFF_AB = 0
_OFF_AC = _OFF_AB + CONV_W
_OFF_AH = _OFF_AC + CONV_W
_OFF_Q = _OFF_AH + CONV_W
_OFF_K = _OFF_Q + ATTN_W
_OFF_V = _OFF_K + ATTN_W
_OFF_CU = _OFF_V + ATTN_W
_OFF_CV = _OFF_CU + SGU_W
_OFF_END = _OFF_CV + SGU_W

V7X_LANES = 128
V7X_SUBLANES = 8
V7X_MXU_DIM = 256
V7X_VMEM_LIMIT = 56 * 1024 * 1024

TM_MIX = 512
TM_POST = 256
T_ATTN = 256
HEADS_PER_STEP = 4
FF_CHUNK = 1024

_F32 = jnp.float32
_BF16 = jnp.bfloat16
_LOG2_E = 1.4426950408889634


def _rms_norm_rows(x, g):
    ms = jnp.mean(x * x, axis=-1, keepdims=True)
    return x * lax.rsqrt(ms + EPS) * g


def _group_mean_square(x):
    r = lax.broadcasted_iota(jnp.int32, (V7X_MXU_DIM, V7X_MXU_DIM), 0) // HEAD_DIM
    c = lax.broadcasted_iota(jnp.int32, (V7X_MXU_DIM, V7X_MXU_DIM), 1) // HEAD_DIM
    ones_bd = (r == c).astype(_BF16)
    x2 = (x * x).astype(_BF16)
    parts = []
    for j in range(x.shape[1] // V7X_MXU_DIM):
        parts.append(jnp.dot(x2[:, j * V7X_MXU_DIM:(j + 1) * V7X_MXU_DIM], ones_bd,
                             preferred_element_type=_F32))
    ms = parts[0] if len(parts) == 1 else jnp.concatenate(parts, axis=1)
    return ms * (1.0 / HEAD_DIM)


def _gelu(x):
    return 0.5 * x * (1.0 + lax.erf(x * (2.0 ** -0.5)))


def _mix_in_kernel(h_ref, g1_ref, win_ref, convw_ref, gq_ref, gk_ref, gv_ref, sw_ref, sb_ref,
                   ya_ref, q_ref, k_ref, v_ref, yc_ref, xs_ref):
    tm = h_ref.shape[1]
    s = pl.program_id(1)
    hn = _rms_norm_rows(h_ref[0], g1_ref[...]).astype(_BF16)

    def proj(a, b):
        return jnp.dot(hn, win_ref[:, a:b], preferred_element_type=_F32)

    a_b = proj(_OFF_AB, _OFF_AC)
    x = proj(_OFF_AC, _OFF_AH) * proj(_OFF_AH, _OFF_Q)

    @pl.when(s == 0)
    def _():
        xs_ref[0:V7X_SUBLANES, :] = jnp.zeros((V7X_SUBLANES, CONV_W), _F32)

    @pl.when(s > 0)
    def _():
        xs_ref[0:V7X_SUBLANES, :] = xs_ref[tm:tm + V7X_SUBLANES, :]

    xs_ref[V7X_SUBLANES:tm + V7X_SUBLANES, :] = x
    cw = convw_ref[...]
    conv = (cw[0:1, :] * xs_ref[V7X_SUBLANES - 2:tm + V7X_SUBLANES - 2, :]
            + cw[1:2, :] * xs_ref[V7X_SUBLANES - 1:tm + V7X_SUBLANES - 1, :]
            + cw[2:3, :] * x)
    ya_ref[0] = (a_b * conv).astype(_BF16)

    q = proj(_OFF_Q, _OFF_K)
    q_scale = gq_ref[...] * (HEAD_DIM ** -0.5 * _LOG2_E)
    q_ref[0] = (q * lax.rsqrt(_group_mean_square(q) + EPS) * q_scale).astype(_BF16)
    k = proj(_OFF_K, _OFF_V)
    k_ref[0] = (k * lax.rsqrt(_group_mean_square(k) + EPS) * gk_ref[...]).astype(_BF16)
    v_ref[0] = proj(_OFF_V, _OFF_CU).astype(_BF16)

    u = _gelu(proj(_OFF_CU, _OFF_CV))
    cv = _gelu(proj(_OFF_CV, _OFF_END))
    vn = (cv * lax.rsqrt(_group_mean_square(cv) + EPS) * gv_ref[...]).astype(_BF16)
    tr = lax.broadcasted_iota(jnp.int32, (CHUNK, CHUNK), 0)
    tc = lax.broadcasted_iota(jnp.int32, (CHUNK, CHUNK), 1)
    w_tril = [jnp.where(tr >= tc, sw_ref[g], 0.0).astype(_BF16) for g in range(SGU_HEADS)]
    lane_group = lax.broadcasted_iota(jnp.int32, (CHUNK, SGU_W), 1) // HEAD_DIM
    bias = sb_ref[...]
    for c in range(tm // CHUNK):
        rows = slice(c * CHUNK, (c + 1) * CHUNK)
        vc = vn[rows, :]
        sv = jnp.dot(w_tril[0], vc, preferred_element_type=_F32)
        for g in range(1, SGU_HEADS):
            sv = jnp.where(lane_group == g, jnp.dot(w_tril[g], vc, preferred_element_type=_F32), sv)
        yc_ref[0, rows, :] = (u[rows, :] * (sv + bias)).astype(_BF16)


def _sb_attn_kernel(q_ref, k_ref, v_ref, o_ref, vt_ref, acc_ref, c_ref):
    t = q_ref.shape[1]
    width = q_ref.shape[2]
    n_heads = width // HEAD_DIM
    i = pl.program_id(2)

    @pl.when(i == 0)
    def _():
        for jb in range(vt_ref.shape[0]):
            vt_ref[jb] = v_ref[0, jb * t:(jb + 1) * t, :].T

    q = q_ref[0]
    head_of_lane = lax.broadcasted_iota(jnp.int32, (t, width), 1) // HEAD_DIM
    q_stack = jnp.concatenate(
        [jnp.where(head_of_lane == h, q, jnp.zeros_like(q)) for h in range(n_heads)], axis=0)
    key = lax.broadcasted_iota(jnp.int32, (t, t), 0)
    query = lax.broadcasted_iota(jnp.int32, (t, t), 1)
    causal = jnp.concatenate([key < query] * n_heads, axis=1)
    prefix = (query >= key).astype(_BF16)
    prefix2 = jnp.concatenate([prefix, prefix], axis=1)

    acc_ref[...] = jnp.zeros_like(acc_ref)
    c_ref[...] = jnp.zeros_like(c_ref)

    def block(jb, masked):
        kb = k_ref[0, pl.ds(pl.multiple_of(jb * t, t), t), :]
        z = lax.dot_general(kb, q_stack, (((1,), (1,)), ((), ())), preferred_element_type=_F32)
        neg_abs = lax.bitcast_convert_type(
            lax.bitcast_convert_type(z, jnp.uint32) | jnp.uint32(0x80000000), _F32)
        sp = jnp.maximum(z, 0.0) + jnp.log(1.0 + jnp.exp2(neg_abs)) * _LOG2_E
        if masked:
            sp = jnp.where(causal, sp, 0.0)
        hi = sp.astype(_BF16)
        lo = (sp - hi.astype(_F32)).astype(_BF16)
        ssum = jnp.dot(prefix2, jnp.concatenate([hi, lo], axis=0), preferred_element_type=_F32)
        c = c_ref[...]
        w = jnp.exp2(z - ssum - c)
        if masked:
            w = jnp.where(causal, w, 0.0)
        c_ref[...] = c + ssum[0:1, :]
        pv = jnp.dot(vt_ref[jb], w.astype(_BF16), preferred_element_type=_F32)
        acc_ref[...] += jnp.concatenate(
            [pv[h * HEAD_DIM:(h + 1) * HEAD_DIM, h * t:(h + 1) * t] for h in range(n_heads)], axis=0)

    block(i, True)

    def body(jj, carry):
        block(i - 1 - jj, False)
        return carry

    lax.fori_loop(0, i, body, 0)
    o_ref[0] = acc_ref[...].T.astype(o_ref.dtype)


def _post_kernel(h_ref, ya_ref, yb_ref, yc_ref, p_ref, wout_ref, g2_ref, w1_ref, w2_ref, g3_ref,
                 wg_ref, wp_ref, o_ref):
    y = jnp.concatenate([ya_ref[0], yb_ref[0], yc_ref[0]], axis=1)
    h = h_ref[0] + jnp.dot(y, wout_ref[...], preferred_element_type=_F32)
    hn = _rms_norm_rows(h, g2_ref[...]).astype(_BF16)
    for c in range(w1_ref.shape[1] // FF_CHUNK):
        cols = slice(c * FF_CHUNK, (c + 1) * FF_CHUNK)
        f = jnp.maximum(jnp.dot(hn, w1_ref[:, cols], preferred_element_type=_F32), 0.0)
        h = h + jnp.dot((f * f).astype(_BF16), w2_ref[cols, :], preferred_element_type=_F32)
    hn = _rms_norm_rows(h, g3_ref[...]).astype(_BF16)
    gate = jax.nn.sigmoid(jnp.dot(hn, wg_ref[...], preferred_element_type=_F32))
    emb = jnp.dot(p_ref[0].astype(_BF16), wp_ref[...], preferred_element_type=_F32)
    o_ref[0] = h + gate * emb


def _const_spec(shape):
    return pl.BlockSpec(shape, lambda *_: (0,) * len(shape), pipeline_mode=pl.Buffered(1))


def _mix_in(h, g1, w_in, conv_w, gq, gk, gv, sgu_w, sgu_bias):
    bsz, s_len, d = h.shape
    tm = TM_MIX
    tok = lambda w: pl.BlockSpec((1, tm, w), lambda b, s: (b, s, 0))
    out = lambda w: jax.ShapeDtypeStruct((bsz, s_len, w), _BF16)
    return pl.pallas_call(
        _mix_in_kernel,
        grid=(bsz, s_len // tm),
        in_specs=[tok(d), _const_spec(g1.shape), _const_spec(w_in.shape), _const_spec(conv_w.shape),
                  _const_spec(gq.shape), _const_spec(gk.shape), _const_spec(gv.shape),
                  _const_spec(sgu_w.shape), _const_spec(sgu_bias.shape)],
        out_specs=[tok(CONV_W), tok(ATTN_W), tok(ATTN_W), tok(ATTN_W), tok(SGU_W)],
        out_shape=[out(CONV_W), out(ATTN_W), out(ATTN_W), out(ATTN_W), out(SGU_W)],
        scratch_shapes=[pltpu.VMEM((tm + V7X_SUBLANES, CONV_W), _F32)],
        compiler_params=pltpu.CompilerParams(
            dimension_semantics=("arbitrary", "arbitrary"), vmem_limit_bytes=V7X_VMEM_LIMIT),
        name="mix_in",
    )(h, g1, w_in, conv_w, gq, gk, gv, sgu_w, sgu_bias)


def _sb_attn(q, k, v):
    bsz, s_len, _ = q.shape
    t = T_ATTN
    width = HEADS_PER_STEP * HEAD_DIM
    return pl.pallas_call(
        _sb_attn_kernel,
        grid=(bsz, ATTN_W // width, s_len // t),
        in_specs=[pl.BlockSpec((1, t, width), lambda b, hg, i: (b, i, hg)),
                  pl.BlockSpec((1, s_len, width), lambda b, hg, i: (b, 0, hg)),
                  pl.BlockSpec((1, s_len, width), lambda b, hg, i: (b, 0, hg))],
        out_specs=pl.BlockSpec((1, t, width), lambda b, hg, i: (b, i, hg)),
        out_shape=jax.ShapeDtypeStruct(q.shape, _BF16),
        scratch_shapes=[pltpu.VMEM((s_len // t, width, t), _BF16), pltpu.VMEM((width, t), _F32),
                        pltpu.VMEM((1, HEADS_PER_STEP * t), _F32)],
        compiler_params=pltpu.CompilerParams(
            dimension_semantics=("arbitrary", "arbitrary", "arbitrary"), vmem_limit_bytes=V7X_VMEM_LIMIT),
        name="sb_attn",
    )(q, k, v)


def _post(h, ya, yb, yc, p, w_out, g2, w1, w2, g3, wg, wp):
    bsz, s_len, d = h.shape
    tm = TM_POST
    tok = lambda w: pl.BlockSpec((1, tm, w), lambda b, s: (b, s, 0))
    return pl.pallas_call(
        _post_kernel,
        grid=(bsz, s_len // tm),
        in_specs=[tok(d), tok(CONV_W), tok(ATTN_W), tok(SGU_W), tok(p.shape[-1]),
                  _const_spec(w_out.shape), _const_spec(g2.shape), _const_spec(w1.shape),
                  _const_spec(w2.shape), _const_spec(g3.shape), _const_spec(wg.shape), _const_spec(wp.shape)],
        out_specs=tok(d),
        out_shape=jax.ShapeDtypeStruct(h.shape, h.dtype),
        compiler_params=pltpu.CompilerParams(
            dimension_semantics=("arbitrary", "arbitrary"), vmem_limit_bytes=V7X_VMEM_LIMIT),
        name="post",
    )(h, ya, yb, yc, p, w_out, g2, w1, w2, g3, wg, wp)


def kernel(x, p, norm1_g, w_in, conv_w, q_norm_g, k_norm_g, sgu_norm_g, sgu_w, sgu_b, w_out, norm2_g,
           w_ff1, w_ff2, norm3_g, w_ple_gate, w_ple_proj):
    depth = w_in.shape[0]
    assert w_in.shape[2] == _OFF_END and x.shape[1] % TM_MIX == 0 and x.shape[1] % T_ATTN == 0
    row = lambda a: a.reshape(1, -1)
    h = x
    for i in range(depth):
        gq = row(jnp.tile(q_norm_g[i], ATTN_HEADS))
        gk = row(jnp.tile(k_norm_g[i], ATTN_HEADS))
        sgu_bias = jnp.repeat(sgu_b[i].T, HEAD_DIM, axis=1)
        ya, q, k, v, yc = _mix_in(h, row(norm1_g[i]), w_in[i].astype(_BF16), conv_w[i], gq, gk,
                                  row(sgu_norm_g[i]), sgu_w[i], sgu_bias)
        yb = _sb_attn(q, k, v)
        h = _post(h, ya, yb, yc, p[i], w_out[i].astype(_BF16), row(norm2_g[i]), w_ff1[i].astype(_BF16),
                  w_ff2[i].astype(_BF16), row(norm3_g[i]), w_ple_gate[i].astype(_BF16),
                  w_ple_proj[i].astype(_BF16))
    return h
```

```python
import functools

import jax
import jax.numpy as jnp
from jax import lax
from jax.experimental import pallas as pl
from jax.experimental.pallas import tpu as pltpu

HEAD_DIM = 64
CONV_HEADS = 4
ATTN_HEADS = 8
SGU_HEADS = 4
CONV_W = CONV_HEADS * HEAD_DIM
ATTN_W = ATTN_HEADS * HEAD_DIM
SGU_W = SGU_HEADS * HEAD_DIM
CONV_WIDTH = 3
CHUNK = 128
EPS = 1e-6

_OFF_AB = 0
_OFF_AC = _OFF_AB + CONV_W
_OFF_AH = _OFF_AC + CONV_W
_OFF_Q = _OFF_AH + CONV_W
_OFF_K = _OFF_Q + ATTN_W
_OFF_V = _OFF_K + ATTN_W
_OFF_CU = _OFF_V + ATTN_W
_OFF_CV = _OFF_CU + SGU_W
_OFF_END = _OFF_CV + SGU_W

V7X_LANES = 128
V7X_SUBLANES = 8
V7X_MXU_DIM = 256
V7X_VMEM_LIMIT = 56 * 1024 * 1024

TM_MIX = 512
TM_POST = 256
T_ATTN = 256
HEADS_PER_STEP = 4
FF_CHUNK = 1024

_F32 = jnp.float32
_BF16 = jnp.bfloat16
_LOG2_E = 1.4426950408889634
_EXP2_CLAMP = 64.0
_EXP2_UNDERFLOW = 160.0


def _rms_norm_rows(x, g):
    ms = jnp.mean(x * x, axis=-1, keepdims=True)
    return x * lax.rsqrt(ms + EPS) * g


def _group_mean_square(x):
    r = lax.broadcasted_iota(jnp.int32, (V7X_MXU_DIM, V7X_MXU_DIM), 0) // HEAD_DIM
    c = lax.broadcasted_iota(jnp.int32, (V7X_MXU_DIM, V7X_MXU_DIM), 1) // HEAD_DIM
    ones_bd = (r == c).astype(_BF16)
    x2 = (x * x).astype(_BF16)
    parts = []
    for j in range(x.shape[1] // V7X_MXU_DIM):
        parts.append(jnp.dot(x2[:, j * V7X_MXU_DIM:(j + 1) * V7X_MXU_DIM], ones_bd,
                             preferred_element_type=_F32))
    ms = parts[0] if len(parts) == 1 else jnp.concatenate(parts, axis=1)
    return ms * (1.0 / HEAD_DIM)


def _gelu(x):
    return 0.5 * x * (1.0 + lax.erf(x * (2.0 ** -0.5)))


def _mix_in_kernel(h_ref, g1_ref, win_ref, convw_ref, gq_ref, gk_ref, gv_ref, sw_ref, sb_ref,
                   ya_ref, q_ref, k_ref, v_ref, yc_ref, xs_ref):
    tm = h_ref.shape[1]
    s = pl.program_id(1)
    hn = _rms_norm_rows(h_ref[0], g1_ref[...]).astype(_BF16)

    def proj(a, b):
        return jnp.dot(hn, win_ref[:, a:b], preferred_element_type=_F32)

    a_b = proj(_OFF_AB, _OFF_AC)
    x = proj(_OFF_AC, _OFF_AH) * proj(_OFF_AH, _OFF_Q)

    @pl.when(s == 0)
    def _():
        xs_ref[0:V7X_SUBLANES, :] = jnp.zeros((V7X_SUBLANES, CONV_W), _F32)

    @pl.when(s > 0)
    def _():
        xs_ref[0:V7X_SUBLANES, :] = xs_ref[tm:tm + V7X_SUBLANES, :]

    xs_ref[V7X_SUBLANES:tm + V7X_SUBLANES, :] = x
    cw = convw_ref[...]
    conv = (cw[0:1, :] * xs_ref[V7X_SUBLANES - 2:tm + V7X_SUBLANES - 2, :]
            + cw[1:2, :] * xs_ref[V7X_SUBLANES - 1:tm + V7X_SUBLANES - 1, :]
            + cw[2:3, :] * x)
    ya_ref[0] = (a_b * conv).astype(_BF16)

    q = proj(_OFF_Q, _OFF_K)
    q_scale = gq_ref[...] * (HEAD_DIM ** -0.5 * _LOG2_E)
    q_ref[0] = (q * lax.rsqrt(_group_mean_square(q) + EPS) * q_scale).astype(_BF16)
    k = proj(_OFF_K, _OFF_V)
    k_ref[0] = (k * lax.rsqrt(_group_mean_square(k) + EPS) * gk_ref[...]).astype(_BF16)
    v_ref[0] = proj(_OFF_V, _OFF_CU).astype(_BF16)

    u = _gelu(proj(_OFF_CU, _OFF_CV))
    cv = _gelu(proj(_OFF_CV, _OFF_END))
    vn = (cv * lax.rsqrt(_group_mean_square(cv) + EPS) * gv_ref[...]).astype(_BF16)
    tr = lax.broadcasted_iota(jnp.int32, (CHUNK, CHUNK), 0)
    tc = lax.broadcasted_iota(jnp.int32, (CHUNK, CHUNK), 1)
    w_tril = [jnp.where(tr >= tc, sw_ref[g], 0.0).astype(_BF16) for g in range(SGU_HEADS)]
    lane_group = lax.broadcasted_iota(jnp.int32, (CHUNK, SGU_W), 1) // HEAD_DIM
    bias = sb_ref[...]
    for c in range(tm // CHUNK):
        rows = slice(c * CHUNK, (c + 1) * CHUNK)
        vc = vn[rows, :]
        sv = jnp.dot(w_tril[0], vc, preferred_element_type=_F32)
        for g in range(1, SGU_HEADS):
            sv = jnp.where(lane_group == g, jnp.dot(w_tril[g], vc, preferred_element_type=_F32), sv)
        yc_ref[0, rows, :] = (u[rows, :] * (sv + bias)).astype(_BF16)


def _sb_attn_kernel(q_ref, k_ref, v_ref, o_ref, vt_ref, acc_ref, c_ref):
    t = q_ref.shape[1]
    width = q_ref.shape[2]
    n_heads = width // HEAD_DIM
    i = pl.program_id(2)

    @pl.when(i == 0)
    def _():
        for jb in range(vt_ref.shape[0]):
            vt_ref[jb] = v_ref[0, jb * t:(jb + 1) * t, :].T

    q = q_ref[0]
    head_of_lane = lax.broadcasted_iota(jnp.int32, (t, width), 1) // HEAD_DIM
    q_stack = jnp.concatenate(
        [jnp.where(head_of_lane == h, q, jnp.zeros_like(q)) for h in range(n_heads)], axis=0)
    key = lax.broadcasted_iota(jnp.int32, (t, t), 0)
    query = lax.broadcasted_iota(jnp.int32, (t, t), 1)
    causal = jnp.concatenate([key < query] * n_heads, axis=1)
    prefix = (query >= key).astype(_BF16)
    prefix2 = jnp.concatenate([prefix, prefix], axis=1)

    acc_ref[...] = jnp.zeros_like(acc_ref)
    c_ref[...] = jnp.zeros_like(c_ref)

    def block(jb, masked):
        kb = k_ref[0, pl.ds(pl.multiple_of(jb * t, t), t), :]
        z = lax.dot_general(kb, q_stack, (((1,), (1,)), ((), ())), preferred_element_type=_F32)
        sp = jnp.maximum(z, jnp.log(1.0 + jnp.exp2(jnp.minimum(z, _EXP2_CLAMP))) * _LOG2_E)
        if masked:
            sp = jnp.where(causal, sp, 0.0)
        hi = sp.astype(_BF16)
        lo = (sp - hi.astype(_F32)).astype(_BF16)
        ssum = jnp.dot(prefix2, jnp.concatenate([hi, lo], axis=0), preferred_element_type=_F32)
        c = c_ref[...]
        w = jnp.exp2(z - ssum - c)
        if masked:
            w = jnp.where(causal, w, 0.0)
        c_new = c + ssum[0:1, :]
        c_ref[...] = c_new
        pv = jnp.dot(vt_ref[jb], w.astype(_BF16), preferred_element_type=_F32)
        acc_ref[...] += jnp.concatenate(
            [pv[h * HEAD_DIM:(h + 1) * HEAD_DIM, h * t:(h + 1) * t] for h in range(n_heads)], axis=0)
        return jnp.min(c_new)

    def more_blocks(carry):
        jb, c_min = carry
        return jnp.logical_and(jb >= 0, c_min < _EXP2_UNDERFLOW)

    def next_block(carry):
        jb, _ = carry
        return jb - 1, block(jb, False)

    lax.while_loop(more_blocks, next_block, (i - 1, block(i, True)))
    o_ref[0] = acc_ref[...].T.astype(o_ref.dtype)


def _post_kernel(h_ref, ya_ref, yb_ref, yc_ref, p_ref, wout_ref, g2_ref, w1_ref, w2_ref, g3_ref,
                 wg_ref, wp_ref, o_ref):
    y = jnp.concatenate([ya_ref[0], yb_ref[0], yc_ref[0]], axis=1)
    h = h_ref[0] + jnp.dot(y, wout_ref[...], preferred_element_type=_F32)
    hn = _rms_norm_rows(h, g2_ref[...]).astype(_BF16)
    for c in range(w1_ref.shape[1] // FF_CHUNK):
        cols = slice(c * FF_CHUNK, (c + 1) * FF_CHUNK)
        f = jnp.maximum(jnp.dot(hn, w1_ref[:, cols], preferred_element_type=_F32), 0.0)
        h = h + jnp.dot((f * f).astype(_BF16), w2_ref[cols, :], preferred_element_type=_F32)
    hn = _rms_norm_rows(h, g3_ref[...]).astype(_BF16)
    gate = jax.nn.sigmoid(jnp.dot(hn, wg_ref[...], preferred_element_type=_F32))
    emb = jnp.dot(p_ref[0].astype(_BF16), wp_ref[...], preferred_element_type=_F32)
    o_ref[0] = h + gate * emb


def _const_spec(shape):
    return pl.BlockSpec(shape, lambda *_: (0,) * len(shape), pipeline_mode=pl.Buffered(1))


def _mix_in(h, g1, w_in, conv_w, gq, gk, gv, sgu_w, sgu_bias):
    bsz, s_len, d = h.shape
    tm = TM_MIX
    tok = lambda w: pl.BlockSpec((1, tm, w), lambda b, s: (b, s, 0))
    out = lambda w: jax.ShapeDtypeStruct((bsz, s_len, w), _BF16)
    return pl.pallas_call(
        _mix_in_kernel,
        grid=(bsz, s_len // tm),
        in_specs=[tok(d), _const_spec(g1.shape), _const_spec(w_in.shape), _const_spec(conv_w.shape),
                  _const_spec(gq.shape), _const_spec(gk.shape), _const_spec(gv.shape),
                  _const_spec(sgu_w.shape), _const_spec(sgu_bias.shape)],
        out_specs=[tok(CONV_W), tok(ATTN_W), tok(ATTN_W), tok(ATTN_W), tok(SGU_W)],
        out_shape=[out(CONV_W), out(ATTN_W), out(ATTN_W), out(ATTN_W), out(SGU_W)],
        scratch_shapes=[pltpu.VMEM((tm + V7X_SUBLANES, CONV_W), _F32)],
        compiler_params=pltpu.CompilerParams(
            dimension_semantics=("arbitrary", "arbitrary"), vmem_limit_bytes=V7X_VMEM_LIMIT),
        name="mix_in",
    )(h, g1, w_in, conv_w, gq, gk, gv, sgu_w, sgu_bias)


def _sb_attn(q, k, v):
    bsz, s_len, _ = q.shape
    t = T_ATTN
    width = HEADS_PER_STEP * HEAD_DIM
    return pl.pallas_call(
        _sb_attn_kernel,
        grid=(bsz, ATTN_W // width, s_len // t),
        in_specs=[pl.BlockSpec((1, t, width), lambda b, hg, i: (b, i, hg)),
                  pl.BlockSpec((1, s_len, width), lambda b, hg, i: (b, 0, hg)),
                  pl.BlockSpec((1, s_len, width), lambda b, hg, i: (b, 0, hg))],
        out_specs=pl.BlockSpec((1, t, width), lambda b, hg, i: (b, i, hg)),
        out_shape=jax.ShapeDtypeStruct(q.shape, _BF16),
        scratch_shapes=[pltpu.VMEM((s_len // t, width, t), _BF16), pltpu.VMEM((width, t), _F32),
                        pltpu.VMEM((1, HEADS_PER_STEP * t), _F32)],
        compiler_params=pltpu.CompilerParams(
            dimension_semantics=("arbitrary", "arbitrary", "arbitrary"), vmem_limit_bytes=V7X_VMEM_LIMIT),
        name="sb_attn",
    )(q, k, v)


def _post(h, ya, yb, yc, p, w_out, g2, w1, w2, g3, wg, wp):
    bsz, s_len, d = h.shape
    tm = TM_POST
    tok = lambda w: pl.BlockSpec((1, tm, w), lambda b, s: (b, s, 0))
    return pl.pallas_call(
        _post_kernel,
        grid=(bsz, s_len // tm),
        in_specs=[tok(d), tok(CONV_W), tok(ATTN_W), tok(SGU_W), tok(p.shape[-1]),
                  _const_spec(w_out.shape), _const_spec(g2.shape), _const_spec(w1.shape),
                  _const_spec(w2.shape), _const_spec(g3.shape), _const_spec(wg.shape), _const_spec(wp.shape)],
        out_specs=tok(d),
        out_shape=jax.ShapeDtypeStruct(h.shape, h.dtype),
        compiler_params=pltpu.CompilerParams(
            dimension_semantics=("arbitrary", "arbitrary"), vmem_limit_bytes=V7X_VMEM_LIMIT),
        name="post",
    )(h, ya, yb, yc, p, w_out, g2, w1, w2, g3, wg, wp)


def kernel(x, p, norm1_g, w_in, conv_w, q_norm_g, k_norm_g, sgu_norm_g, sgu_w, sgu_b, w_out, norm2_g,
           w_ff1, w_ff2, norm3_g, w_ple_gate, w_ple_proj):
    depth = w_in.shape[0]
    assert w_in.shape[2] == _OFF_END and x.shape[1] % TM_MIX == 0 and x.shape[1] % T_ATTN == 0
    row = lambda a: a.reshape(1, -1)
    h = x
    for i in range(depth):
        gq = row(jnp.tile(q_norm_g[i], ATTN_HEADS))
        gk = row(jnp.tile(k_norm_g[i], ATTN_HEADS))
        sgu_bias = jnp.repeat(sgu_b[i].T, HEAD_DIM, axis=1)
        ya, q, k, v, yc = _mix_in(h, row(norm1_g[i]), w_in[i].astype(_BF16), conv_w[i], gq, gk,
                                  row(sgu_norm_g[i]), sgu_w[i], sgu_bias)
        yb = _sb_attn(q, k, v)
        h = _post(h, ya, yb, yc, p[i], w_out[i].astype(_BF16), row(norm2_g[i]), w_ff1[i].astype(_BF16),
                  w_ff2[i].astype(_BF16), row(norm3_g[i]), w_ple_gate[i].astype(_BF16),
                  w_ple_proj[i].astype(_BF16))
    return h
```

```python
import functools

import jax
import jax.numpy as jnp
from jax import lax
from jax.experimental import pallas as pl
from jax.experimental.pallas import tpu as pltpu

HEAD_DIM = 64
CONV_HEADS = 4
ATTN_HEADS = 8
SGU_HEADS = 4
CONV_W = CONV_HEADS * HEAD_DIM
ATTN_W = ATTN_HEADS * HEAD_DIM
SGU_W = SGU_HEADS * HEAD_DIM
CONV_WIDTH = 3
CHUNK = 128
EPS = 1e-6

_OFF_AB = 0
_OFF_AC = _OFF_AB + CONV_W
_OFF_AH = _OFF_AC + CONV_W
_OFF_Q = _OFF_AH + CONV_W
_OFF_K = _OFF_Q + ATTN_W
_OFF_V = _OFF_K + ATTN_W
_OFF_CU = _OFF_V + ATTN_W
_OFF_CV = _OFF_CU + SGU_W
_OFF_END = _OFF_CV + SGU_W

V7X_LANES = 128
V7X_SUBLANES = 8
V7X_MXU_DIM = 256
V7X_VMEM_LIMIT = 56 * 1024 * 1024

TM_MIX = 512
TM_POST = 512
T_ATTN = 256
HEADS_PER_STEP = 4
FF_CHUNK = 1024

_F32 = jnp.float32
_BF16 = jnp.bfloat16
_LOG2_E = 1.4426950408889634
_EXP2_CLAMP = 64.0
_EXP2_UNDERFLOW = 160.0


def _rms_norm_rows(x, g):
    ms = jnp.mean(x * x, axis=-1, keepdims=True)
    return x * lax.rsqrt(ms + EPS) * g


def _group_mean_square(x):
    r = lax.broadcasted_iota(jnp.int32, (V7X_MXU_DIM, V7X_MXU_DIM), 0) // HEAD_DIM
    c = lax.broadcasted_iota(jnp.int32, (V7X_MXU_DIM, V7X_MXU_DIM), 1) // HEAD_DIM
    ones_bd = (r == c).astype(_BF16)
    x2 = (x * x).astype(_BF16)
    parts = []
    for j in range(x.shape[1] // V7X_MXU_DIM):
        parts.append(jnp.dot(x2[:, j * V7X_MXU_DIM:(j + 1) * V7X_MXU_DIM], ones_bd,
                             preferred_element_type=_F32))
    ms = parts[0] if len(parts) == 1 else jnp.concatenate(parts, axis=1)
    return ms * (1.0 / HEAD_DIM)


def _gelu(x):
    return 0.5 * x * (1.0 + lax.erf(x * (2.0 ** -0.5)))


def _mix_in_kernel(h_ref, g1_ref, win_ref, convw_ref, gq_ref, gk_ref, gv_ref, sw_ref, sb_ref,
                   ya_ref, q_ref, k_ref, v_ref, yc_ref, xs_ref):
    tm = h_ref.shape[1]
    s = pl.program_id(1)
    hn = _rms_norm_rows(h_ref[0], g1_ref[...]).astype(_BF16)

    def proj(a, b):
        return jnp.dot(hn, win_ref[:, a:b], preferred_element_type=_F32)

    a_b = proj(_OFF_AB, _OFF_AC)
    x = proj(_OFF_AC, _OFF_AH) * proj(_OFF_AH, _OFF_Q)

    @pl.when(s == 0)
    def _():
        xs_ref[0:V7X_SUBLANES, :] = jnp.zeros((V7X_SUBLANES, CONV_W), _F32)

    @pl.when(s > 0)
    def _():
        xs_ref[0:V7X_SUBLANES, :] = xs_ref[tm:tm + V7X_SUBLANES, :]

    xs_ref[V7X_SUBLANES:tm + V7X_SUBLANES, :] = x
    cw = convw_ref[...]
    conv = (cw[0:1, :] * xs_ref[V7X_SUBLANES - 2:tm + V7X_SUBLANES - 2, :]
            + cw[1:2, :] * xs_ref[V7X_SUBLANES - 1:tm + V7X_SUBLANES - 1, :]
            + cw[2:3, :] * x)
    ya_ref[0] = (a_b * conv).astype(_BF16)

    q = proj(_OFF_Q, _OFF_K)
    q_scale = gq_ref[...] * (HEAD_DIM ** -0.5 * _LOG2_E)
    q_ref[0] = (q * lax.rsqrt(_group_mean_square(q) + EPS) * q_scale).astype(_BF16)
    k = proj(_OFF_K, _OFF_V)
    k_ref[0] = (k * lax.rsqrt(_group_mean_square(k) + EPS) * gk_ref[...]).astype(_BF16)
    v_ref[0] = proj(_OFF_V, _OFF_CU).astype(_BF16)

    u = _gelu(proj(_OFF_CU, _OFF_CV))
    cv = _gelu(proj(_OFF_CV, _OFF_END))
    vn = (cv * lax.rsqrt(_group_mean_square(cv) + EPS) * gv_ref[...]).astype(_BF16)
    tr = lax.broadcasted_iota(jnp.int32, (CHUNK, CHUNK), 0)
    tc = lax.broadcasted_iota(jnp.int32, (CHUNK, CHUNK), 1)
    w_tril = [jnp.where(tr >= tc, sw_ref[g], 0.0).astype(_BF16) for g in range(SGU_HEADS)]
    lane_group = lax.broadcasted_iota(jnp.int32, (CHUNK, SGU_W), 1) // HEAD_DIM
    bias = sb_ref[...]
    for c in range(tm // CHUNK):
        rows = slice(c * CHUNK, (c + 1) * CHUNK)
        vc = vn[rows, :]
        sv = jnp.dot(w_tril[0], vc, preferred_element_type=_F32)
        for g in range(1, SGU_HEADS):
            sv = jnp.where(lane_group == g, jnp.dot(w_tril[g], vc, preferred_element_type=_F32), sv)
        yc_ref[0, rows, :] = (u[rows, :] * (sv + bias)).astype(_BF16)


def _sb_attn_kernel(q_ref, k_ref, v_ref, o_ref, vt_ref, acc_ref, c_ref):
    t = q_ref.shape[1]
    width = q_ref.shape[2]
    n_heads = width // HEAD_DIM
    i = pl.program_id(2)

    @pl.when(i == 0)
    def _():
        for jb in range(vt_ref.shape[0]):
            vt_ref[jb] = v_ref[0, jb * t:(jb + 1) * t, :].T

    q = q_ref[0]
    head_of_lane = lax.broadcasted_iota(jnp.int32, (t, width), 1) // HEAD_DIM
    q_stack = jnp.concatenate(
        [jnp.where(head_of_lane == h, q, jnp.zeros_like(q)) for h in range(n_heads)], axis=0)
    key = lax.broadcasted_iota(jnp.int32, (t, t), 0)
    query = lax.broadcasted_iota(jnp.int32, (t, t), 1)
    causal = jnp.concatenate([key < query] * n_heads, axis=1)
    prefix = (query >= key).astype(_BF16)

    acc_ref[...] = jnp.zeros_like(acc_ref)
    c_ref[...] = jnp.zeros_like(c_ref)

    def blocks(first, count, diagonal):
        kb = k_ref[0, pl.ds(pl.multiple_of(first * t, t), count * t), :]
        z = lax.dot_general(kb, q_stack, (((1,), (1,)), ((), ())), preferred_element_type=_F32)
        sp = jnp.maximum(z, jnp.log(1.0 + jnp.exp2(jnp.minimum(z, _EXP2_CLAMP))) * _LOG2_E)
        c = c_ref[...]
        weights = []
        for n in reversed(range(count)):
            rows = slice(n * t, (n + 1) * t)
            masked = diagonal and n == count - 1
            sp_n = jnp.where(causal, sp[rows], 0.0) if masked else sp[rows]
            ssum = jnp.dot(prefix, sp_n.astype(_BF16), preferred_element_type=_F32)
            w = jnp.exp2(z[rows] - ssum - c)
            weights.append(jnp.where(causal, w, 0.0) if masked else w)
            c = c + ssum[0:1, :]
        c_ref[...] = c
        w_all = jnp.concatenate(weights[::-1], axis=0).astype(_BF16)
        vt = jnp.concatenate([vt_ref[first + n] for n in range(count)], axis=1)
        pv = jnp.dot(vt, w_all, preferred_element_type=_F32)
        acc_ref[...] += jnp.concatenate(
            [pv[h * HEAD_DIM:(h + 1) * HEAD_DIM, h * t:(h + 1) * t] for h in range(n_heads)], axis=0)
        return jnp.min(c)

    @pl.when(i == 0)
    def _():
        blocks(0, 1, True)

    @pl.when(i > 0)
    def _():
        def more_blocks(carry):
            jb, c_min = carry
            return jnp.logical_and(jb >= 0, c_min < _EXP2_UNDERFLOW)

        def next_block(carry):
            jb, _ = carry
            return jb - 1, blocks(jb, 1, False)

        lax.while_loop(more_blocks, next_block, (i - 2, blocks(i - 1, 2, True)))

    o_ref[0] = acc_ref[...].T.astype(o_ref.dtype)


def _post_kernel(h_ref, ya_ref, yb_ref, yc_ref, p_ref, wout_ref, g2_ref, w1_ref, w2_ref, g3_ref,
                 wg_ref, wp_ref, o_ref):
    y = jnp.concatenate([ya_ref[0], yb_ref[0], yc_ref[0]], axis=1)
    h = h_ref[0] + jnp.dot(y, wout_ref[...], preferred_element_type=_F32)
    hn = _rms_norm_rows(h, g2_ref[...]).astype(_BF16)
    for c in range(w1_ref.shape[1] // FF_CHUNK):
        cols = slice(c * FF_CHUNK, (c + 1) * FF_CHUNK)
        f = jnp.maximum(jnp.dot(hn, w1_ref[:, cols], preferred_element_type=_F32), 0.0)
        h = h + jnp.dot((f * f).astype(_BF16), w2_ref[cols, :], preferred_element_type=_F32)
    hn = _rms_norm_rows(h, g3_ref[...]).astype(_BF16)
    gate = jax.nn.sigmoid(jnp.dot(hn, wg_ref[...], preferred_element_type=_F32))
    emb = jnp.dot(p_ref[0].astype(_BF16), wp_ref[...], preferred_element_type=_F32)
    o_ref[0] = h + gate * emb


def _layer_spec(arr, layer):
    zeros = (0,) * (arr.ndim - 1)
    return pl.BlockSpec((None,) + arr.shape[1:], lambda *_: (layer,) + zeros, pipeline_mode=pl.Buffered(1))


def _mix_in(layer, h, g1, w_in, conv_w, gq, gk, gv, sgu_w, sgu_bias):
    bsz, s_len, d = h.shape
    tm = TM_MIX
    tok = lambda w: pl.BlockSpec((1, tm, w), lambda b, s: (b, s, 0))
    out = lambda w: jax.ShapeDtypeStruct((bsz, s_len, w), _BF16)
    params = (g1, w_in, conv_w, gq, gk, gv, sgu_w, sgu_bias)
    return pl.pallas_call(
        _mix_in_kernel,
        grid=(bsz, s_len // tm),
        in_specs=[tok(d)] + [_layer_spec(a, layer) for a in params],
        out_specs=[tok(CONV_W), tok(ATTN_W), tok(ATTN_W), tok(ATTN_W), tok(SGU_W)],
        out_shape=[out(CONV_W), out(ATTN_W), out(ATTN_W), out(ATTN_W), out(SGU_W)],
        scratch_shapes=[pltpu.VMEM((tm + V7X_SUBLANES, CONV_W), _F32)],
        compiler_params=pltpu.CompilerParams(
            dimension_semantics=("arbitrary", "arbitrary"), vmem_limit_bytes=V7X_VMEM_LIMIT),
        name="mix_in",
    )(h, *params)


def _sb_attn(q, k, v):
    bsz, s_len, _ = q.shape
    t = T_ATTN
    width = HEADS_PER_STEP * HEAD_DIM
    return pl.pallas_call(
        _sb_attn_kernel,
        grid=(bsz, ATTN_W // width, s_len // t),
        in_specs=[pl.BlockSpec((1, t, width), lambda b, hg, i: (b, i, hg)),
                  pl.BlockSpec((1, s_len, width), lambda b, hg, i: (b, 0, hg)),
                  pl.BlockSpec((1, s_len, width), lambda b, hg, i: (b, 0, hg))],
        out_specs=pl.BlockSpec((1, t, width), lambda b, hg, i: (b, i, hg)),
        out_shape=jax.ShapeDtypeStruct(q.shape, _BF16),
        scratch_shapes=[pltpu.VMEM((s_len // t, width, t), _BF16), pltpu.VMEM((width, t), _F32),
                        pltpu.VMEM((1, HEADS_PER_STEP * t), _F32)],
        compiler_params=pltpu.CompilerParams(
            dimension_semantics=("arbitrary", "arbitrary", "arbitrary"), vmem_limit_bytes=V7X_VMEM_LIMIT),
        name="sb_attn",
    )(q, k, v)


def _post(layer, h, ya, yb, yc, p, w_out, g2, w1, w2, g3, wg, wp):
    bsz, s_len, d = h.shape
    tm = TM_POST
    tok = lambda w: pl.BlockSpec((1, tm, w), lambda b, s: (b, s, 0))
    p_spec = pl.BlockSpec((None, 1, tm, p.shape[-1]), lambda b, s: (layer, b, s, 0))
    params = (w_out, g2, w1, w2, g3, wg, wp)
    return pl.pallas_call(
        _post_kernel,
        grid=(bsz, s_len // tm),
        in_specs=[tok(d), tok(CONV_W), tok(ATTN_W), tok(SGU_W), p_spec] + [_layer_spec(a, layer) for a in params],
        out_specs=tok(d),
        out_shape=jax.ShapeDtypeStruct(h.shape, h.dtype),
        compiler_params=pltpu.CompilerParams(
            dimension_semantics=("arbitrary", "arbitrary"), vmem_limit_bytes=V7X_VMEM_LIMIT),
        name="post",
    )(h, ya, yb, yc, p, *params)


def kernel(x, p, norm1_g, w_in, conv_w, q_norm_g, k_norm_g, sgu_norm_g, sgu_w, sgu_b, w_out, norm2_g,
           w_ff1, w_ff2, norm3_g, w_ple_gate, w_ple_proj):
    depth = w_in.shape[0]
    assert w_in.shape[2] == _OFF_END and x.shape[1] % TM_MIX == 0 and x.shape[1] % T_ATTN == 0
    rows = lambda a: a.reshape(depth, 1, -1)
    bf16 = lambda a: a.astype(_BF16)
    gq = rows(jnp.tile(q_norm_g, (1, ATTN_HEADS)))
    gk = rows(jnp.tile(k_norm_g, (1, ATTN_HEADS)))
    sgu_bias = jnp.repeat(jnp.swapaxes(sgu_b, 1, 2), HEAD_DIM, axis=2)
    mix_params = (rows(norm1_g), bf16(w_in), conv_w, gq, gk, rows(sgu_norm_g), sgu_w, sgu_bias)
    post_params = (bf16(w_out), rows(norm2_g), bf16(w_ff1), bf16(w_ff2), rows(norm3_g), bf16(w_ple_gate),
                   bf16(w_ple_proj))
    h = x
    for layer in range(depth):
        ya, q, k, v, yc = _mix_in(layer, h, *mix_params)
        yb = _sb_attn(q, k, v)
        h = _post(layer, h, ya, yb, yc, p, *post_params)
    return h
```

```python
import functools

import jax
import jax.numpy as jnp
from jax import lax
from jax.experimental import pallas as pl
from jax.experimental.pallas import tpu as pltpu

HEAD_DIM = 64
CONV_HEADS = 4
ATTN_HEADS = 8
SGU_HEADS = 4
CONV_W = CONV_HEADS * HEAD_DIM
ATTN_W = ATTN_HEADS * HEAD_DIM
SGU_W = SGU_HEADS * HEAD_DIM
CONV_WIDTH = 3
CHUNK = 128
EPS = 1e-6

_OFF_AB = 0
_OFF_AC = _OFF_AB + CONV_W
_OFF_AH = _OFF_AC + CONV_W
_OFF_Q = _OFF_AH + CONV_W
_OFF_K = _OFF_Q + ATTN_W
_OFF_V = _OFF_K + ATTN_W
_OFF_CU = _OFF_V + ATTN_W
_OFF_CV = _OFF_CU + SGU_W
_OFF_END = _OFF_CV + SGU_W

V7X_LANES = 128
V7X_SUBLANES = 8
V7X_MXU_DIM = 256
V7X_VMEM_LIMIT = 56 * 1024 * 1024

TM_MIX = 512
TM_POST = 512
T_ATTN = 256
HEADS_PER_STEP = 4
Q_BLOCKS_PER_STEP = 2
FF_CHUNK = 1024

_F32 = jnp.float32
_BF16 = jnp.bfloat16
_LOG2_E = 1.4426950408889634
_EXP2_CLAMP = 64.0
_EXP2_UNDERFLOW = 160.0


def _rms_norm_rows(x, g):
    ms = jnp.mean(x * x, axis=-1, keepdims=True)
    return x * lax.rsqrt(ms + EPS) * g


def _group_mean_square(x):
    r = lax.broadcasted_iota(jnp.int32, (V7X_MXU_DIM, V7X_MXU_DIM), 0) // HEAD_DIM
    c = lax.broadcasted_iota(jnp.int32, (V7X_MXU_DIM, V7X_MXU_DIM), 1) // HEAD_DIM
    ones_bd = (r == c).astype(_BF16)
    x2 = (x * x).astype(_BF16)
    parts = []
    for j in range(x.shape[1] // V7X_MXU_DIM):
        parts.append(jnp.dot(x2[:, j * V7X_MXU_DIM:(j + 1) * V7X_MXU_DIM], ones_bd,
                             preferred_element_type=_F32))
    ms = parts[0] if len(parts) == 1 else jnp.concatenate(parts, axis=1)
    return ms * (1.0 / HEAD_DIM)


def _gelu(x):
    return 0.5 * x * (1.0 + lax.erf(x * (2.0 ** -0.5)))


def _mix_in_kernel(h_ref, g1_ref, win_ref, convw_ref, gq_ref, gk_ref, gv_ref, sw_ref, sb_ref,
                   ya_ref, q_ref, k_ref, v_ref, yc_ref, xs_ref):
    tm = h_ref.shape[1]

    @pl.when(pl.program_id(1) == 0)
    def _():
        xs_ref[tm:tm + V7X_SUBLANES, :] = jnp.zeros((V7X_SUBLANES, CONV_W), _F32)

    hn = _rms_norm_rows(h_ref[0], g1_ref[...]).astype(_BF16)

    def proj(a, b):
        return jnp.dot(hn, win_ref[:, a:b], preferred_element_type=_F32)

    u = _gelu(proj(_OFF_CU, _OFF_CV))
    cv = _gelu(proj(_OFF_CV, _OFF_END))
    vn = (cv * lax.rsqrt(_group_mean_square(cv) + EPS) * gv_ref[...]).astype(_BF16)
    tr = lax.broadcasted_iota(jnp.int32, (CHUNK, CHUNK), 0)
    tc = lax.broadcasted_iota(jnp.int32, (CHUNK, CHUNK), 1)
    w_tril = jnp.concatenate(
        [jnp.where(tr >= tc, sw_ref[g], 0.0).astype(_BF16) for g in range(SGU_HEADS)], axis=1)
    lane_group = lax.broadcasted_iota(jnp.int32, (CHUNK, SGU_W), 1) // HEAD_DIM
    bias = sb_ref[...]
    for c in range(tm // CHUNK):
        rows = slice(c * CHUNK, (c + 1) * CHUNK)
        vc = vn[rows, :]
        v_groups = jnp.concatenate(
            [jnp.where(lane_group == g, vc, jnp.zeros_like(vc)) for g in range(SGU_HEADS)], axis=0)
        sv = jnp.dot(w_tril, v_groups, preferred_element_type=_F32)
        yc_ref[0, rows, :] = (u[rows, :] * (sv + bias)).astype(_BF16)

    q = proj(_OFF_Q, _OFF_K)
    q_scale = gq_ref[...] * (HEAD_DIM ** -0.5 * _LOG2_E)
    q_ref[0] = (q * lax.rsqrt(_group_mean_square(q) + EPS) * q_scale).astype(_BF16)
    k = proj(_OFF_K, _OFF_V)
    k_ref[0] = (k * lax.rsqrt(_group_mean_square(k) + EPS) * gk_ref[...]).astype(_BF16)
    v_ref[0] = proj(_OFF_V, _OFF_CU).astype(_BF16)

    a_b = proj(_OFF_AB, _OFF_AC)
    x = proj(_OFF_AC, _OFF_AH) * proj(_OFF_AH, _OFF_Q)
    xs_ref[0:V7X_SUBLANES, :] = xs_ref[tm:tm + V7X_SUBLANES, :]
    xs_ref[V7X_SUBLANES:tm + V7X_SUBLANES, :] = x
    cw = convw_ref[...]
    conv = (cw[0:1, :] * xs_ref[V7X_SUBLANES - 2:tm + V7X_SUBLANES - 2, :]
            + cw[1:2, :] * xs_ref[V7X_SUBLANES - 1:tm + V7X_SUBLANES - 1, :]
            + cw[2:3, :] * x)
    ya_ref[0] = (a_b * conv).astype(_BF16)


def _sb_attn_kernel(q_ref, k_ref, v_ref, o_ref, vt_ref, acc_ref, c_ref):
    n_q, width, t = acc_ref.shape
    n_heads = width // HEAD_DIM
    i = pl.program_id(2)

    @pl.when(i == 0)
    def _():
        for jb in range(vt_ref.shape[0]):
            vt_ref[jb] = v_ref[0, jb * t:(jb + 1) * t, :].T

    head_of_lane = lax.broadcasted_iota(jnp.int32, (t, width), 1) // HEAD_DIM
    q_stacks = []
    for r in range(n_q):
        q = q_ref[0, r * t:(r + 1) * t, :]
        q_stacks.append(jnp.concatenate(
            [jnp.where(head_of_lane == h, q, jnp.zeros_like(q)) for h in range(n_heads)], axis=0))
    key = lax.broadcasted_iota(jnp.int32, (t, t), 0)
    query = lax.broadcasted_iota(jnp.int32, (t, t), 1)
    causal = jnp.concatenate([key < query] * n_heads, axis=1)
    prefix = (query >= key).astype(_BF16)

    acc_ref[...] = jnp.zeros_like(acc_ref)
    c_ref[...] = jnp.zeros_like(c_ref)

    def run(jobs):
        zs = []
        for r, first, count, _ in jobs:
            kb = k_ref[0, pl.ds(pl.multiple_of(first * t, t), count * t), :]
            zs.append(lax.dot_general(kb, q_stacks[r], (((1,), (1,)), ((), ())), preferred_element_type=_F32))
        sps = [jnp.maximum(z, jnp.log(1.0 + jnp.exp2(jnp.minimum(z, _EXP2_CLAMP))) * _LOG2_E) for z in zs]
        ws = []
        for (r, _, count, diagonal), z, sp in zip(jobs, zs, sps):
            c = c_ref[r]
            weights = []
            for n in reversed(range(count)):
                rows = slice(n * t, (n + 1) * t)
                masked = diagonal and n == count - 1
                sp_n = jnp.where(causal, sp[rows], 0.0) if masked else sp[rows]
                ssum = jnp.dot(prefix, sp_n.astype(_BF16), preferred_element_type=_F32)
                w = jnp.exp2(z[rows] - ssum - c)
                weights.append(jnp.where(causal, w, 0.0) if masked else w)
                c = c + ssum[0:1, :]
            c_ref[r] = c
            ws.append(jnp.concatenate(weights[::-1], axis=0).astype(_BF16))
        for (r, first, count, _), w_all in zip(jobs, ws):
            vt = jnp.concatenate([vt_ref[first + n] for n in range(count)], axis=1)
            pv = jnp.dot(vt, w_all, preferred_element_type=_F32)
            acc_ref[r] += jnp.concatenate(
                [pv[h * HEAD_DIM:(h + 1) * HEAD_DIM, h * t:(h + 1) * t] for h in range(n_heads)], axis=0)

    @pl.when(i == 0)
    def _():
        run([(0, 0, 1, True)] + [(r, r - 1, 2, True) for r in range(1, n_q)])

    @pl.when(i > 0)
    def _():
        run([(r, n_q * i + r - 1, 2, True) for r in range(n_q)])

    for r in range(n_q):
        def more_blocks(carry):
            jb, c_min = carry
            return jnp.logical_and(jb >= 0, c_min < _EXP2_UNDERFLOW)

        def next_block(carry, r=r):
            jb, _ = carry
            run([(r, jb, 1, False)])
            return jb - 1, jnp.min(c_ref[r])

        lax.while_loop(more_blocks, next_block, (n_q * i + r - 2, jnp.min(c_ref[r])))
        o_ref[0, r * t:(r + 1) * t, :] = acc_ref[r].T.astype(o_ref.dtype)


def _post_kernel(h_ref, ya_ref, yb_ref, yc_ref, p_ref, wout_ref, g2_ref, w1_ref, w2_ref, g3_ref,
                 wg_ref, wp_ref, o_ref):
    y = jnp.concatenate([ya_ref[0], yb_ref[0], yc_ref[0]], axis=1)
    h = h_ref[0] + jnp.dot(y, wout_ref[...], preferred_element_type=_F32)
    hn = _rms_norm_rows(h, g2_ref[...]).astype(_BF16)
    for c in range(w1_ref.shape[1] // FF_CHUNK):
        cols = slice(c * FF_CHUNK, (c + 1) * FF_CHUNK)
        f = jnp.maximum(jnp.dot(hn, w1_ref[:, cols], preferred_element_type=_F32), 0.0)
        h = h + jnp.dot((f * f).astype(_BF16), w2_ref[cols, :], preferred_element_type=_F32)
    hn = _rms_norm_rows(h, g3_ref[...]).astype(_BF16)
    gate = jax.nn.sigmoid(jnp.dot(hn, wg_ref[...], preferred_element_type=_F32))
    emb = jnp.dot(p_ref[0].astype(_BF16), wp_ref[...], preferred_element_type=_F32)
    o_ref[0] = h + gate * emb


def _layer_spec(arr, layer):
    zeros = (0,) * (arr.ndim - 1)
    return pl.BlockSpec((None,) + arr.shape[1:], lambda *_: (layer,) + zeros, pipeline_mode=pl.Buffered(1))


def _mix_in(layer, h, g1, w_in, conv_w, gq, gk, gv, sgu_w, sgu_bias):
    bsz, s_len, d = h.shape
    tm = TM_MIX
    tok = lambda w: pl.BlockSpec((1, tm, w), lambda b, s: (b, s, 0))
    out = lambda w: jax.ShapeDtypeStruct((bsz, s_len, w), _BF16)
    params = (g1, w_in, conv_w, gq, gk, gv, sgu_w, sgu_bias)
    return pl.pallas_call(
        _mix_in_kernel,
        grid=(bsz, s_len // tm),
        in_specs=[tok(d)] + [_layer_spec(a, layer) for a in params],
        out_specs=[tok(CONV_W), tok(ATTN_W), tok(ATTN_W), tok(ATTN_W), tok(SGU_W)],
        out_shape=[out(CONV_W), out(ATTN_W), out(ATTN_W), out(ATTN_W), out(SGU_W)],
        scratch_shapes=[pltpu.VMEM((tm + V7X_SUBLANES, CONV_W), _F32)],
        compiler_params=pltpu.CompilerParams(
            dimension_semantics=("arbitrary", "arbitrary"), vmem_limit_bytes=V7X_VMEM_LIMIT),
        name="mix_in",
    )(h, *params)


def _sb_attn(q, k, v):
    bsz, s_len, _ = q.shape
    t = T_ATTN
    tq = Q_BLOCKS_PER_STEP * t
    width = HEADS_PER_STEP * HEAD_DIM
    return pl.pallas_call(
        _sb_attn_kernel,
        grid=(bsz, ATTN_W // width, s_len // tq),
        in_specs=[pl.BlockSpec((1, tq, width), lambda b, hg, i: (b, i, hg)),
                  pl.BlockSpec((1, s_len, width), lambda b, hg, i: (b, 0, hg)),
                  pl.BlockSpec((1, s_len, width), lambda b, hg, i: (b, 0, hg))],
        out_specs=pl.BlockSpec((1, tq, width), lambda b, hg, i: (b, i, hg)),
        out_shape=jax.ShapeDtypeStruct(q.shape, _BF16),
        scratch_shapes=[pltpu.VMEM((s_len // t, width, t), _BF16),
                        pltpu.VMEM((Q_BLOCKS_PER_STEP, width, t), _F32),
                        pltpu.VMEM((Q_BLOCKS_PER_STEP, 1, HEADS_PER_STEP * t), _F32)],
        compiler_params=pltpu.CompilerParams(
            dimension_semantics=("arbitrary", "arbitrary", "arbitrary"), vmem_limit_bytes=V7X_VMEM_LIMIT),
        name="sb_attn",
    )(q, k, v)


def _post(layer, h, ya, yb, yc, p, w_out, g2, w1, w2, g3, wg, wp):
    bsz, s_len, d = h.shape
    tm = TM_POST
    tok = lambda w: pl.BlockSpec((1, tm, w), lambda b, s: (b, s, 0))
    p_spec = pl.BlockSpec((None, 1, tm, p.shape[-1]), lambda b, s: (layer, b, s, 0))
    params = (w_out, g2, w1, w2, g3, wg, wp)
    return pl.pallas_call(
        _post_kernel,
        grid=(bsz, s_len // tm),
        in_specs=[tok(d), tok(CONV_W), tok(ATTN_W), tok(SGU_W), p_spec] + [_layer_spec(a, layer) for a in params],
        out_specs=tok(d),
        out_shape=jax.ShapeDtypeStruct(h.shape, h.dtype),
        compiler_params=pltpu.CompilerParams(
            dimension_semantics=("arbitrary", "arbitrary"), vmem_limit_bytes=V7X_VMEM_LIMIT),
        name="post",
    )(h, ya, yb, yc, p, *params)


def kernel(x, p, norm1_g, w_in, conv_w, q_norm_g, k_norm_g, sgu_norm_g, sgu_w, sgu_b, w_out, norm2_g,
           w_ff1, w_ff2, norm3_g, w_ple_gate, w_ple_proj):
    depth = w_in.shape[0]
    assert w_in.shape[2] == _OFF_END and x.shape[1] % TM_MIX == 0 and x.shape[1] % (Q_BLOCKS_PER_STEP * T_ATTN) == 0
    rows = lambda a: a.reshape(depth, 1, -1)
    bf16 = lambda a: a.astype(_BF16)
    gq = rows(jnp.tile(q_norm_g, (1, ATTN_HEADS)))
    gk = rows(jnp.tile(k_norm_g, (1, ATTN_HEADS)))
    sgu_bias = jnp.repeat(jnp.swapaxes(sgu_b, 1, 2), HEAD_DIM, axis=2)
    mix_params = (rows(norm1_g), bf16(w_in), conv_w, gq, gk, rows(sgu_norm_g), sgu_w, sgu_bias)
    post_params = (bf16(w_out), rows(norm2_g), bf16(w_ff1), bf16(w_ff2), rows(norm3_g), bf16(w_ple_gate),
                   bf16(w_ple_proj))
    h = x
    for layer in range(depth):
        ya, q, k, v, yc = _mix_in(layer, h, *mix_params)
        yb = _sb_attn(q, k, v)
        h = _post(layer, h, ya, yb, yc, p, *post_params)
    return h
```

```python
import functools

import jax
import jax.numpy as jnp
from jax import lax
from jax.experimental import pallas as pl
from jax.experimental.pallas import tpu as pltpu

HEAD_DIM = 64
CONV_HEADS = 4
ATTN_HEADS = 8
SGU_HEADS = 4
CONV_W = CONV_HEADS * HEAD_DIM
ATTN_W = ATTN_HEADS * HEAD_DIM
SGU_W = SGU_HEADS * HEAD_DIM
CONV_WIDTH = 3
CHUNK = 128
EPS = 1e-6

_OFF_AB = 0
_OFF_AC = _OFF_AB + CONV_W
_OFF_AH = _OFF_AC + CONV_W
_OFF_Q = _OFF_AH + CONV_W
_OFF_K = _OFF_Q + ATTN_W
_OFF_V = _OFF_K + ATTN_W
_OFF_CU = _OFF_V + ATTN_W
_OFF_CV = _OFF_CU + SGU_W
_OFF_END = _OFF_CV + SGU_W

V7X_LANES = 128
V7X_SUBLANES = 8
V7X_MXU_DIM = 256
V7X_VMEM_LIMIT = 56 * 1024 * 1024

TM_MIX = 1024
TM_POST = 512
T_ATTN = 256
HEADS_PER_STEP = 4
Q_BLOCKS_PER_STEP = 4
FF_CHUNK = 1024

_F32 = jnp.float32
_BF16 = jnp.bfloat16
_LOG2_E = 1.4426950408889634
_EXP2_CLAMP = 64.0
_EXP2_UNDERFLOW = 160.0


def _rms_norm_rows(x, g):
    ms = jnp.mean(x * x, axis=-1, keepdims=True)
    return x * lax.rsqrt(ms + EPS) * g


def _group_mean_square(x):
    r = lax.broadcasted_iota(jnp.int32, (V7X_MXU_DIM, V7X_MXU_DIM), 0) // HEAD_DIM
    c = lax.broadcasted_iota(jnp.int32, (V7X_MXU_DIM, V7X_MXU_DIM), 1) // HEAD_DIM
    ones_bd = (r == c).astype(_BF16)
    x2 = (x * x).astype(_BF16)
    parts = []
    for j in range(x.shape[1] // V7X_MXU_DIM):
        parts.append(jnp.dot(x2[:, j * V7X_MXU_DIM:(j + 1) * V7X_MXU_DIM], ones_bd,
                             preferred_element_type=_F32))
    ms = parts[0] if len(parts) == 1 else jnp.concatenate(parts, axis=1)
    return ms * (1.0 / HEAD_DIM)


def _gelu(x):
    return 0.5 * x * (1.0 + lax.erf(x * (2.0 ** -0.5)))


def _mix_in_kernel(h_ref, g1_ref, win_ref, convw_ref, gq_ref, gk_ref, gv_ref, sw_ref, sb_ref,
                   ya_ref, q_ref, k_ref, v_ref, yc_ref, xs_ref):
    tm = h_ref.shape[1]

    @pl.when(pl.program_id(1) == 0)
    def _():
        xs_ref[tm:tm + V7X_SUBLANES, :] = jnp.zeros((V7X_SUBLANES, CONV_W), _F32)

    hn = _rms_norm_rows(h_ref[0], g1_ref[...]).astype(_BF16)

    def proj(a, b):
        return jnp.dot(hn, win_ref[:, a:b], preferred_element_type=_F32)

    u = _gelu(proj(_OFF_CU, _OFF_CV))
    cv = _gelu(proj(_OFF_CV, _OFF_END))
    vn = (cv * lax.rsqrt(_group_mean_square(cv) + EPS) * gv_ref[...]).astype(_BF16)
    tr = lax.broadcasted_iota(jnp.int32, (CHUNK, CHUNK), 0)
    tc = lax.broadcasted_iota(jnp.int32, (CHUNK, CHUNK), 1)
    w_tril = jnp.concatenate(
        [jnp.where(tr >= tc, sw_ref[g], 0.0).astype(_BF16) for g in range(SGU_HEADS)], axis=1)
    lane_group = lax.broadcasted_iota(jnp.int32, (CHUNK, SGU_W), 1) // HEAD_DIM
    bias = sb_ref[...]
    for c in range(tm // CHUNK):
        rows = slice(c * CHUNK, (c + 1) * CHUNK)
        vc = vn[rows, :]
        v_groups = jnp.concatenate(
            [jnp.where(lane_group == g, vc, jnp.zeros_like(vc)) for g in range(SGU_HEADS)], axis=0)
        sv = jnp.dot(w_tril, v_groups, preferred_element_type=_F32)
        yc_ref[0, rows, :] = (u[rows, :] * (sv + bias)).astype(_BF16)

    q = proj(_OFF_Q, _OFF_K)
    q_scale = gq_ref[...] * (HEAD_DIM ** -0.5 * _LOG2_E)
    q_ref[0] = (q * lax.rsqrt(_group_mean_square(q) + EPS) * q_scale).astype(_BF16)
    k = proj(_OFF_K, _OFF_V)
    k_ref[0] = (k * lax.rsqrt(_group_mean_square(k) + EPS) * gk_ref[...]).astype(_BF16)
    v_ref[0] = proj(_OFF_V, _OFF_CU).astype(_BF16)

    a_b = proj(_OFF_AB, _OFF_AC)
    x = proj(_OFF_AC, _OFF_AH) * proj(_OFF_AH, _OFF_Q)
    xs_ref[0:V7X_SUBLANES, :] = xs_ref[tm:tm + V7X_SUBLANES, :]
    xs_ref[V7X_SUBLANES:tm + V7X_SUBLANES, :] = x
    cw = convw_ref[...]
    conv = (cw[0:1, :] * xs_ref[V7X_SUBLANES - 2:tm + V7X_SUBLANES - 2, :]
            + cw[1:2, :] * xs_ref[V7X_SUBLANES - 1:tm + V7X_SUBLANES - 1, :]
            + cw[2:3, :] * x)
    ya_ref[0] = (a_b * conv).astype(_BF16)


def _sb_attn_kernel(q_ref, k_ref, v_ref, o_ref, vt_ref, acc_ref, c_ref):
    n_q, width, t = acc_ref.shape
    n_heads = width // HEAD_DIM
    i = pl.program_id(2)

    @pl.when(i == 0)
    def _():
        for jb in range(vt_ref.shape[0]):
            vt_ref[jb] = v_ref[0, jb * t:(jb + 1) * t, :].T

    head_of_lane = lax.broadcasted_iota(jnp.int32, (t, width), 1) // HEAD_DIM
    q_stacks = []
    for r in range(n_q):
        q = q_ref[0, r * t:(r + 1) * t, :]
        q_stacks.append(jnp.concatenate(
            [jnp.where(head_of_lane == h, q, jnp.zeros_like(q)) for h in range(n_heads)], axis=0))
    key = lax.broadcasted_iota(jnp.int32, (t, t), 0)
    query = lax.broadcasted_iota(jnp.int32, (t, t), 1)
    causal = jnp.concatenate([key < query] * n_heads, axis=1)
    prefix = (query >= key).astype(_BF16)

    acc_ref[...] = jnp.zeros_like(acc_ref)
    c_ref[...] = jnp.zeros_like(c_ref)

    def run(jobs):
        zs = []
        for r, first, count, _ in jobs:
            kb = k_ref[0, pl.ds(pl.multiple_of(first * t, t), count * t), :]
            zs.append(lax.dot_general(kb, q_stacks[r], (((1,), (1,)), ((), ())), preferred_element_type=_F32))
        sps = [jnp.maximum(z, jnp.log(1.0 + jnp.exp2(jnp.minimum(z, _EXP2_CLAMP))) * _LOG2_E) for z in zs]
        ws = []
        for (r, _, count, diagonal), z, sp in zip(jobs, zs, sps):
            c = c_ref[r]
            weights = []
            for n in reversed(range(count)):
                rows = slice(n * t, (n + 1) * t)
                masked = diagonal and n == count - 1
                sp_n = jnp.where(causal, sp[rows], 0.0) if masked else sp[rows]
                ssum = jnp.dot(prefix, sp_n.astype(_BF16), preferred_element_type=_F32)
                w = jnp.exp2(z[rows] - ssum - c)
                weights.append(jnp.where(causal, w, 0.0) if masked else w)
                c = c + ssum[0:1, :]
            c_ref[r] = c
            ws.append(jnp.concatenate(weights[::-1], axis=0).astype(_BF16))
        for (r, first, count, _), w_all in zip(jobs, ws):
            vt = jnp.concatenate([vt_ref[first + n] for n in range(count)], axis=1)
            pv = jnp.dot(vt, w_all, preferred_element_type=_F32)
            acc_ref[r] += jnp.concatenate(
                [pv[h * HEAD_DIM:(h + 1) * HEAD_DIM, h * t:(h + 1) * t] for h in range(n_heads)], axis=0)

    @pl.when(i == 0)
    def _():
        run([(0, 0, 1, True)] + [(r, r - 1, 2, True) for r in range(1, n_q)])

    @pl.when(i > 0)
    def _():
        run([(r, n_q * i + r - 1, 2, True) for r in range(n_q)])

    for r in range(n_q):
        def more_blocks(carry):
            jb, c_min = carry
            return jnp.logical_and(jb >= 0, c_min < _EXP2_UNDERFLOW)

        def next_block(carry, r=r):
            jb, _ = carry
            run([(r, jb, 1, False)])
            return jb - 1, jnp.min(c_ref[r])

        lax.while_loop(more_blocks, next_block, (n_q * i + r - 2, jnp.min(c_ref[r])))
        o_ref[0, r * t:(r + 1) * t, :] = acc_ref[r].T.astype(o_ref.dtype)


def _post_kernel(h_ref, ya_ref, yb_ref, yc_ref, p_ref, wout_ref, g2_ref, w1_ref, w2_ref, g3_ref,
                 wg_ref, wp_ref, o_ref):
    y = jnp.concatenate([ya_ref[0], yb_ref[0], yc_ref[0]], axis=1)
    h = h_ref[0] + jnp.dot(y, wout_ref[...], preferred_element_type=_F32)
    hn = _rms_norm_rows(h, g2_ref[...]).astype(_BF16)
    for c in range(w1_ref.shape[1] // FF_CHUNK):
        cols = slice(c * FF_CHUNK, (c + 1) * FF_CHUNK)
        f = jnp.maximum(jnp.dot(hn, w1_ref[:, cols], preferred_element_type=_F32), 0.0)
        h = h + jnp.dot((f * f).astype(_BF16), w2_ref[cols, :], preferred_element_type=_F32)
    hn = _rms_norm_rows(h, g3_ref[...]).astype(_BF16)
    gate = jax.nn.sigmoid(jnp.dot(hn, wg_ref[...], preferred_element_type=_F32))
    emb = jnp.dot(p_ref[0].astype(_BF16), wp_ref[...], preferred_element_type=_F32)
    o_ref[0] = h + gate * emb


def _layer_spec(arr, layer):
    zeros = (0,) * (arr.ndim - 1)
    return pl.BlockSpec((None,) + arr.shape[1:], lambda *_: (layer,) + zeros, pipeline_mode=pl.Buffered(1))


def _mix_in(layer, h, g1, w_in, conv_w, gq, gk, gv, sgu_w, sgu_bias):
    bsz, s_len, d = h.shape
    tm = TM_MIX
    tok = lambda w: pl.BlockSpec((1, tm, w), lambda b, s: (b, s, 0))
    out = lambda w: jax.ShapeDtypeStruct((bsz, s_len, w), _BF16)
    params = (g1, w_in, conv_w, gq, gk, gv, sgu_w, sgu_bias)
    return pl.pallas_call(
        _mix_in_kernel,
        grid=(bsz, s_len // tm),
        in_specs=[tok(d)] + [_layer_spec(a, layer) for a in params],
        out_specs=[tok(CONV_W), tok(ATTN_W), tok(ATTN_W), tok(ATTN_W), tok(SGU_W)],
        out_shape=[out(CONV_W), out(ATTN_W), out(ATTN_W), out(ATTN_W), out(SGU_W)],
        scratch_shapes=[pltpu.VMEM((tm + V7X_SUBLANES, CONV_W), _F32)],
        compiler_params=pltpu.CompilerParams(
            dimension_semantics=("arbitrary", "arbitrary"), vmem_limit_bytes=V7X_VMEM_LIMIT),
        name="mix_in",
    )(h, *params)


def _sb_attn(q, k, v):
    bsz, s_len, _ = q.shape
    t = T_ATTN
    tq = Q_BLOCKS_PER_STEP * t
    width = HEADS_PER_STEP * HEAD_DIM
    return pl.pallas_call(
        _sb_attn_kernel,
        grid=(bsz, ATTN_W // width, s_len // tq),
        in_specs=[pl.BlockSpec((1, tq, width), lambda b, hg, i: (b, i, hg)),
                  pl.BlockSpec((1, s_len, width), lambda b, hg, i: (b, 0, hg)),
                  pl.BlockSpec((1, s_len, width), lambda b, hg, i: (b, 0, hg))],
        out_specs=pl.BlockSpec((1, tq, width), lambda b, hg, i: (b, i, hg)),
        out_shape=jax.ShapeDtypeStruct(q.shape, _BF16),
        scratch_shapes=[pltpu.VMEM((s_len // t, width, t), _BF16),
                        pltpu.VMEM((Q_BLOCKS_PER_STEP, width, t), _F32),
                        pltpu.VMEM((Q_BLOCKS_PER_STEP, 1, HEADS_PER_STEP * t), _F32)],
        compiler_params=pltpu.CompilerParams(
            dimension_semantics=("arbitrary", "arbitrary", "arbitrary"), vmem_limit_bytes=V7X_VMEM_LIMIT),
        name="sb_attn",
    )(q, k, v)


def _post(layer, h, ya, yb, yc, p, w_out, g2, w1, w2, g3, wg, wp):
    bsz, s_len, d = h.shape
    tm = TM_POST
    tok = lambda w: pl.BlockSpec((1, tm, w), lambda b, s: (b, s, 0))
    p_spec = pl.BlockSpec((None, 1, tm, p.shape[-1]), lambda b, s: (layer, b, s, 0))
    params = (w_out, g2, w1, w2, g3, wg, wp)
    return pl.pallas_call(
        _post_kernel,
        grid=(bsz, s_len // tm),
        in_specs=[tok(d), tok(CONV_W), tok(ATTN_W), tok(SGU_W), p_spec] + [_layer_spec(a, layer) for a in params],
        out_specs=tok(d),
        out_shape=jax.ShapeDtypeStruct(h.shape, h.dtype),
        compiler_params=pltpu.CompilerParams(
            dimension_semantics=("arbitrary", "arbitrary"), vmem_limit_bytes=V7X_VMEM_LIMIT),
        name="post",
    )(h, ya, yb, yc, p, *params)


def kernel(x, p, norm1_g, w_in, conv_w, q_norm_g, k_norm_g, sgu_norm_g, sgu_w, sgu_b, w_out, norm2_g,
           w_ff1, w_ff2, norm3_g, w_ple_gate, w_ple_proj):
    depth = w_in.shape[0]
    assert w_in.shape[2] == _OFF_END and x.shape[1] % TM_MIX == 0 and x.shape[1] % (Q_BLOCKS_PER_STEP * T_ATTN) == 0
    rows = lambda a: a.reshape(depth, 1, -1)
    bf16 = lambda a: a.astype(_BF16)
    gq = rows(jnp.tile(q_norm_g, (1, ATTN_HEADS)))
    gk = rows(jnp.tile(k_norm_g, (1, ATTN_HEADS)))
    sgu_bias = jnp.repeat(jnp.swapaxes(sgu_b, 1, 2), HEAD_DIM, axis=2)
    mix_params = (rows(norm1_g), bf16(w_in), conv_w, gq, gk, rows(sgu_norm_g), sgu_w, sgu_bias)
    post_params = (bf16(w_out), rows(norm2_g), bf16(w_ff1), bf16(w_ff2), rows(norm3_g), bf16(w_ple_gate),
                   bf16(w_ple_proj))
    h = x
    for layer in range(depth):
        ya, q, k, v, yc = _mix_in(layer, h, *mix_params)
        yb = _sb_attn(q, k, v)
        h = _post(layer, h, ya, yb, yc, p, *post_params)
    return h
```
